```python
import math
import jax, jax.numpy as jnp
from jax import lax
import numpy as np

D_MODEL = 1024
BATCH = 32
SEQ = 2048
DEPTH = 2
DEC_BATCH = 32
DEC_SEQ = 16
PAST_LEN = 1024

CHUNK = 64
N_HEADS = 8
HEAD_DIM = 64
ATTN_WIDTH = N_HEADS * HEAD_DIM
LRU_WIDTH = D_MODEL - ATTN_WIDTH
LRU_BLOCKS = 8
LRU_BLOCK_DIM = LRU_WIDTH // LRU_BLOCKS
CONV_W = 4
RGLRU_C = 8.0
D_FF = 4 * D_MODEL
Q_BLOCK = 128
EPS = 1e-6
ATTN_SCALE = 1.0 / math.sqrt(HEAD_DIM)
SPLITS = (ATTN_WIDTH, 2 * ATTN_WIDTH, 3 * ATTN_WIDTH, 3 * ATTN_WIDTH + N_HEADS,
          3 * ATTN_WIDTH + N_HEADS + LRU_WIDTH)
IN_WIDTH = 3 * ATTN_WIDTH + N_HEADS + 2 * LRU_WIDTH

kernel_name = "hymba_fox_rglru_streaming_step"


def rms_norm(x, g):
    xf = x.astype(jnp.float32)
    y = xf * lax.rsqrt(jnp.mean(xf * xf, axis=-1, keepdims=True) + EPS)
    return (y * g).astype(x.dtype)


def fox_attend(q, k, v, cq, ck, q_pos, k_pos):
    s = jnp.einsum('bqhd,bkhd->bhqk', q, k, preferred_element_type=jnp.float32) * ATTN_SCALE
    s = s + jnp.transpose(cq, (0, 2, 1))[:, :, :, None] - jnp.transpose(ck, (0, 2, 1))[:, :, None, :]
    mask = k_pos[None, :] <= q_pos[:, None]
    s = jnp.where(mask[None, None], s, -jnp.inf)
    p = jax.nn.softmax(s, axis=-1).astype(v.dtype)
    return jnp.einsum('bhqk,bkhd->bqhd', p, v)


def fox_prompt(q, k, v, logf):
    B, T, H, Dh = q.shape
    c = jnp.cumsum(logf, axis=1)
    nb = T // Q_BLOCK
    qb = jnp.transpose(q.reshape(B, nb, Q_BLOCK, H, Dh), (1, 0, 2, 3, 4))
    cqb = jnp.transpose(c.reshape(B, nb, Q_BLOCK, H), (1, 0, 2, 3))
    posb = jnp.arange(T, dtype=jnp.int32).reshape(nb, Q_BLOCK)
    kpos = jnp.arange(T, dtype=jnp.int32)

    def one_block(args):
        qi, cqi, pi = args
        return fox_attend(qi, k, v, cqi, c, pi, kpos)

    out = lax.map(one_block, (qb, cqb, posb))
    return jnp.transpose(out, (1, 0, 2, 3, 4)).reshape(B, T, H, Dh)


def fox_sample(q, k, v, logf, k_past, v_past, logf_past):
    P = k_past.shape[1]
    T = q.shape[1]
    k_all = jnp.concatenate([k_past.astype(k.dtype), k], axis=1)
    v_all = jnp.concatenate([v_past.astype(v.dtype), v], axis=1)
    c_all = jnp.cumsum(jnp.concatenate([logf_past.astype(jnp.float32), logf], axis=1), axis=1)
    q_pos = P + jnp.arange(T, dtype=jnp.int32)
    k_pos = jnp.arange(P + T, dtype=jnp.int32)
    return fox_attend(q, k_all, v_all, c_all[:, P:], c_all, q_pos, k_pos)


def causal_conv(x, buf, w, b):
    T = x.shape[1]
    xp = jnp.concatenate([buf.astype(x.dtype), x], axis=1)
    y = sum(xp[:, j:j + T] * w[j] for j in range(CONV_W)) + b
    return y.astype(x.dtype), xp[:, -(CONV_W - 1):]


def rg_lru(x, h0, w_ga, b_ga, w_gx, b_gx, lam):
    B, T, W = x.shape
    xb = x.reshape(B, T, LRU_BLOCKS, LRU_BLOCK_DIM)
    r = jax.nn.sigmoid((jnp.einsum('btnc,ncd->btnd', xb, w_ga).reshape(B, T, W) + b_ga).astype(jnp.float32))
    i = jax.nn.sigmoid((jnp.einsum('btnc,ncd->btnd', xb, w_gx).reshape(B, T, W) + b_gx).astype(jnp.float32))
    log_a = -RGLRU_C * r * jax.nn.softplus(-lam.astype(jnp.float32))
    a = jnp.exp(log_a)
    bterm = jnp.sqrt(-jnp.expm1(2.0 * log_a)) * (i * x.astype(jnp.float32))
    bterm = bterm.at[:, 0].add(a[:, 0] * h0.astype(jnp.float32))

    def combine(left, right):
        a1, b1 = left
        a2, b2 = right
        return a1 * a2, a2 * b1 + b2

    _, h = lax.associative_scan(combine, (a, bterm), axis=1)
    return h, h[:, -1]


def hybrid_layer(x, past, ln1, w_in, b_f, q_gain, k_gain, conv_w, conv_b, w_ga, b_ga,
                 w_gx, b_gx, lam, g_ao, g_lo, w_out, ln2, w_up, w_down):
    B, T, _ = x.shape
    h = rms_norm(x, ln1)
    proj = h @ w_in
    q, k, v, f, xr, yg = jnp.split(proj, SPLITS, axis=-1)
    q = rms_norm(q.reshape(B, T, N_HEADS, HEAD_DIM), q_gain)
    k = rms_norm(k.reshape(B, T, N_HEADS, HEAD_DIM), k_gain)
    v = v.reshape(B, T, N_HEADS, HEAD_DIM)
    logf = jax.nn.log_sigmoid((f + b_f).astype(jnp.float32))

    if past is None:
        attn = fox_prompt(q, k, v, logf)
        h0 = jnp.zeros((B, LRU_WIDTH), jnp.float32)
        conv_buf = jnp.zeros((B, CONV_W - 1, LRU_WIDTH), x.dtype)
    else:
        k_past, v_past, logf_past, h0, conv_buf = past
        attn = fox_sample(q, k, v, logf, k_past, v_past, logf_past)

    xc, new_conv = causal_conv(xr, conv_buf, conv_w, conv_b)
    hs, h_last = rg_lru(xc, h0, w_ga, b_ga, w_gx, b_gx, lam)
    lru_out = jax.nn.gelu(yg) * hs.astype(x.dtype)

    mixed = jnp.concatenate([rms_norm(attn.reshape(B, T, ATTN_WIDTH), g_ao),
                             rms_norm(lru_out, g_lo)], axis=-1)
    x = x + mixed @ w_out
    h2 = rms_norm(x, ln2)
    x = x + jnp.square(jax.nn.relu(h2 @ w_up)) @ w_down
    return x, (k, v, logf, h_last, new_conv)


def setup_inputs(seed: int = 0) -> dict:
    key = jax.random.key(seed)
    ks = jax.random.split(key, 32)
    f32 = jnp.float32
    nrm = lambda k, shape, s: jax.random.normal(k, shape, f32) * s
    u = jax.random.uniform(ks[14], (DEPTH, LRU_WIDTH), f32, 0.9, 0.999)
    return {
        "x_prompt": nrm(ks[0], (BATCH, SEQ, D_MODEL), 1.0),
        "x_sample": nrm(ks[1], (DEC_BATCH, DEC_SEQ, D_MODEL), 1.0),
        "cache_k": nrm(ks[2], (DEPTH, DEC_BATCH, PAST_LEN, N_HEADS, HEAD_DIM), 1.0),
        "cache_v": nrm(ks[3], (DEPTH, DEC_BATCH, PAST_LEN, N_HEADS, HEAD_DIM), 1.0),
        "cache_logf": jax.nn.log_sigmoid(3.0 + nrm(ks[4], (DEPTH, DEC_BATCH, PAST_LEN, N_HEADS), 1.0)),
        "state_h": nrm(ks[5], (DEPTH, DEC_BATCH, LRU_WIDTH), 0.5),
        "state_conv": nrm(ks[6], (DEPTH, DEC_BATCH, CONV_W - 1, LRU_WIDTH), 1.0),
        "ln1": 1.0 + nrm(ks[7], (DEPTH, D_MODEL), 0.02),
        "w_in": nrm(ks[8], (DEPTH, D_MODEL, IN_WIDTH), D_MODEL ** -0.5),
        "b_f": 3.0 + nrm(ks[9], (DEPTH, N_HEADS), 0.5),
        "q_gain": 1.0 + nrm(ks[10], (DEPTH, HEAD_DIM), 0.02),
        "k_gain": 1.0 + nrm(ks[11], (DEPTH, HEAD_DIM), 0.02),
        "conv_w": nrm(ks[12], (DEPTH, CONV_W, LRU_WIDTH), CONV_W ** -0.5),
        "conv_b": nrm(ks[13], (DEPTH, LRU_WIDTH), 0.02),
        "w_gate_a": nrm(ks[15], (DEPTH, LRU_BLOCKS, LRU_BLOCK_DIM, LRU_BLOCK_DIM), LRU_BLOCK_DIM ** -0.5),
        "b_gate_a": nrm(ks[16], (DEPTH, LRU_WIDTH), 0.1),
        "w_gate_x": nrm(ks[17], (DEPTH, LRU_BLOCKS, LRU_BLOCK_DIM, LRU_BLOCK_DIM), LRU_BLOCK_DIM ** -0.5),
        "b_gate_x": nrm(ks[18], (DEPTH, LRU_WIDTH), 0.1),
        "lru_lambda": jnp.log(u) - jnp.log1p(-u),
        "g_attn_out": 1.0 + nrm(ks[19], (DEPTH, ATTN_WIDTH), 0.02),
        "g_lru_out": 1.0 + nrm(ks[20], (DEPTH, LRU_WIDTH), 0.02),
        "w_out": nrm(ks[21], (DEPTH, D_MODEL, D_MODEL), D_MODEL ** -0.5),
        "ln2": 1.0 + nrm(ks[22], (DEPTH, D_MODEL), 0.02),
        "w_up": nrm(ks[23], (DEPTH, D_MODEL, D_FF), D_MODEL ** -0.5),
        "w_down": nrm(ks[24], (DEPTH, D_FF, D_MODEL), D_FF ** -0.5),
    }


def reference(x_prompt, x_sample, cache_k, cache_v, cache_logf, state_h, state_conv,
              ln1, w_in, b_f, q_gain, k_gain, conv_w, conv_b, w_gate_a, b_gate_a,
              w_gate_x, b_gate_x, lru_lambda, g_attn_out, g_lru_out, w_out, ln2, w_up, w_down):
    yp, ys = x_prompt, x_sample
    kp, vp, fp, hp, cp = [], [], [], [], []
    kss, vss, fss, hss, css = [], [], [], [], []
    for l in range(DEPTH):
        params = (ln1[l], w_in[l], b_f[l], q_gain[l], k_gain[l], conv_w[l], conv_b[l],
                  w_gate_a[l], b_gate_a[l], w_gate_x[l], b_gate_x[l], lru_lambda[l],
                  g_attn_out[l], g_lru_out[l], w_out[l], ln2[l], w_up[l], w_down[l])
        yp, (k1, v1, f1, h1, c1) = hybrid_layer(yp, None, *params)
        ys, (k2, v2, f2, h2, c2) = hybrid_layer(
            ys, (cache_k[l], cache_v[l], cache_logf[l], state_h[l], state_conv[l]), *params)
        kp.append(k1); vp.append(v1); fp.append(f1); hp.append(h1); cp.append(c1)
        kss.append(k2); vss.append(v2); fss.append(f2); hss.append(h2); css.append(c2)
    return (yp, ys,
            jnp.stack(kp), jnp.stack(vp), jnp.stack(fp), jnp.stack(hp), jnp.stack(cp),
            jnp.stack(kss), jnp.stack(vss), jnp.stack(fss), jnp.stack(hss), jnp.stack(css))
```

```python
import functools
import math

import jax
import jax.numpy as jnp
from jax import lax
from jax.experimental import pallas as pl
from jax.experimental.pallas import tpu as pltpu

D_MODEL = 1024
N_HEADS = 8
HEAD_DIM = 64
ATTN_WIDTH = N_HEADS * HEAD_DIM
LRU_WIDTH = D_MODEL - ATTN_WIDTH
LRU_BLOCKS = 8
LRU_BLOCK_DIM = LRU_WIDTH // LRU_BLOCKS
CONV_W = 4
RGLRU_C = 8.0
D_FF = 4 * D_MODEL
EPS = 1e-6
ATTN_SCALE = 1.0 / math.sqrt(HEAD_DIM)

LANES = 128
SUBLANES = 8
AUG = 16
ROW_TILE = 512
LRU_TILE = 256
VMEM_LIMIT = 56 * 1024 * 1024

BF16 = jnp.bfloat16
F32 = jnp.float32


def _resident(shape):
    nd = len(shape)
    return pl.BlockSpec(shape, lambda *_: (0,) * nd, pipeline_mode=pl.Buffered(1))


def _split3(c):
    hi = c.astype(BF16).astype(F32)
    r = c - hi
    mid = r.astype(BF16).astype(F32)
    lo = (r - mid).astype(BF16).astype(F32)
    return hi, mid, lo


def _log_sigmoid(x):
    return jnp.minimum(x, 0.0) - jnp.log1p(jnp.exp(-jnp.abs(x)))


def _proj_kernel(x_ref, ln1_ref, wt_ref, wn_ref, bf_ref, gq_ref, gk_ref, selq_ref, selk_ref,
                 k_ref, v_ref, logf_ref, kbf_ref, kaug_ref, xr_ref, yg_ref, *rest,
                 seg, carry, prompt):
    if prompt:
        vt_ref, qt_ref, qaugt_ref, c_scr = rest
    else:
        vbf_ref, qn_ref, qaugn_ref, cnat_ref, c_scr = rest
    tm = x_ref.shape[0]

    x = x_ref[...]
    ms = jnp.mean(x * x, axis=-1, keepdims=True)
    h = (x * lax.rsqrt(ms + EPS) * ln1_ref[...]).astype(BF16)

    pt = lax.dot_general(wt_ref[...], h, (((1,), (1,)), ((), ())), preferred_element_type=F32)
    pn = jnp.dot(h, wn_ref[...], preferred_element_type=F32)
    xr_ref[...] = pn[:, :LRU_WIDTH]
    yg_ref[...] = pn[:, LRU_WIDTH:]

    gq = gq_ref[...] * ATTN_SCALE
    gk = gk_ref[...]
    q_heads, k_heads = [], []
    for hd in range(N_HEADS):
        qh = pt[hd * HEAD_DIM:(hd + 1) * HEAD_DIM]
        kh = pt[ATTN_WIDTH + hd * HEAD_DIM:ATTN_WIDTH + (hd + 1) * HEAD_DIM]
        qh = qh * lax.rsqrt(jnp.mean(qh * qh, axis=0, keepdims=True) + EPS) * gq
        kh = kh * lax.rsqrt(jnp.mean(kh * kh, axis=0, keepdims=True) + EPS) * gk
        q_heads.append(qh)
        k_heads.append(kh)
    qt = jnp.concatenate(q_heads, axis=0)
    kt = jnp.concatenate(k_heads, axis=0)
    vt = pt[2 * ATTN_WIDTH:3 * ATTN_WIDTH]
    ft = pt[3 * ATTN_WIDTH:3 * ATTN_WIDTH + N_HEADS]

    logf_t = _log_sigmoid(ft + bf_ref[...])

    pos = lax.broadcasted_iota(jnp.int32, (N_HEADS, tm), 1) & (seg - 1)
    c = logf_t
    d = 1
    while d < seg:
        c = c + jnp.where(pos >= d, pltpu.roll(c, d, axis=1), 0.0)
        d *= 2
    if carry:
        @pl.when(pl.program_id(1) == 0)
        def _():
            c_scr[...] = jnp.zeros_like(c_scr)
        c = c + c_scr[:, 0:1]
        c_scr[...] = jnp.broadcast_to(c[:, tm - 1:tm], c_scr.shape)

    hi, mid, lo = _split3(c)
    parts = jnp.concatenate([hi, mid, lo, jnp.zeros_like(hi)], axis=0).astype(BF16)
    rr = lax.broadcasted_iota(jnp.int32, (N_HEADS * AUG, tm), 0) & (AUG - 1)
    qaug_t = jnp.dot(selq_ref[...], parts, preferred_element_type=F32)
    qaug_t = qaug_t + jnp.where((rr >= 3) & (rr < 6), 1.0, 0.0)
    kaug_t = jnp.dot(selk_ref[...], parts, preferred_element_type=F32)
    kaug_t = kaug_t + jnp.where(rr < 3, 1.0, 0.0)

    k_nat = kt.T
    v_nat = vt.T
    k_ref[...] = k_nat
    v_ref[...] = v_nat
    kbf_ref[...] = k_nat.astype(BF16)
    kaug_ref[...] = kaug_t.T.astype(BF16)

    slab = jnp.concatenate(
        [logf_t, c, jnp.zeros((LANES - 2 * N_HEADS, tm), F32)], axis=0).T
    logf_ref[...] = slab[:, :N_HEADS]

    if prompt:
        vt_ref[...] = vt.astype(BF16)
        qt_ref[...] = qt.astype(BF16)
        qaugt_ref[...] = qaug_t.astype(BF16)
    else:
        vbf_ref[...] = v_nat.astype(BF16)
        qn_ref[...] = qt.T.astype(BF16)
        qaugn_ref[...] = qaug_t.T.astype(BF16)
        cnat_ref[...] = slab[:, N_HEADS:2 * N_HEADS]


def _proj_call(x2d, lw, n_seq, seq_len, prompt):
    rows = n_seq * seq_len
    if prompt:
        tm = ROW_TILE
        nt = seq_len // tm
        grid = (n_seq, nt)
        seg, carry = tm, True
        row_map = lambda b, t: (b * nt + t, 0)
        t_map = lambda b, t: (b, t, 0, 0)
    else:
        tm = rows
        nt = 1
        grid = (1, 1)
        seg, carry = seq_len, False
        row_map = lambda b, t: (0, 0)
    assert seg & (seg - 1) == 0 and rows % tm == 0

    def row_spec(width):
        return pl.BlockSpec((tm, width), row_map)

    in_specs = [
        row_spec(D_MODEL),
        _resident((1, D_MODEL)),
        _resident(lw["wt"].shape),
        _resident(lw["wn"].shape),
        _resident((N_HEADS, 1)),
        _resident((HEAD_DIM, 1)),
        _resident((HEAD_DIM, 1)),
        _resident(lw["selq"].shape),
        _resident(lw["selk"].shape),
    ]
    out_shape = [
        jax.ShapeDtypeStruct((rows, ATTN_WIDTH), F32),
        jax.ShapeDtypeStruct((rows, ATTN_WIDTH), F32),
        jax.ShapeDtypeStruct((rows, N_HEADS), F32),
        jax.ShapeDtypeStruct((rows, ATTN_WIDTH), BF16),
        jax.ShapeDtypeStruct((rows, LANES), BF16),
        jax.ShapeDtypeStruct((rows, LRU_WIDTH), F32),
        jax.ShapeDtypeStruct((rows, LRU_WIDTH), F32),
    ]
    out_specs = [row_spec(ATTN_WIDTH), row_spec(ATTN_WIDTH), row_spec(N_HEADS),
                 row_spec(ATTN_WIDTH), row_spec(LANES), row_spec(LRU_WIDTH), row_spec(LRU_WIDTH)]
    if prompt:
        out_shape += [
            jax.ShapeDtypeStruct((n_seq, nt, ATTN_WIDTH, tm), BF16),
            jax.ShapeDtypeStruct((n_seq, nt, ATTN_WIDTH, tm), BF16),
            jax.ShapeDtypeStruct((n_seq, nt, N_HEADS * AUG, tm), BF16),
        ]
        out_specs += [pl.BlockSpec((None, None, ATTN_WIDTH, tm), t_map),
                      pl.BlockSpec((None, None, ATTN_WIDTH, tm), t_map),
                      pl.BlockSpec((None, None, N_HEADS * AUG, tm), t_map)]
    else:
        out_shape += [
            jax.ShapeDtypeStruct((rows, ATTN_WIDTH), BF16),
            jax.ShapeDtypeStruct((rows, ATTN_WIDTH), BF16),
            jax.ShapeDtypeStruct((rows, LANES), BF16),
            jax.ShapeDtypeStruct((rows, N_HEADS), F32),
        ]
        out_specs += [row_spec(ATTN_WIDTH), row_spec(ATTN_WIDTH), row_spec(LANES), row_spec(N_HEADS)]

    return pl.pallas_call(
        functools.partial(_proj_kernel, seg=seg, carry=carry, prompt=prompt),
        grid=grid,
        in_specs=in_specs,
        out_specs=out_specs,
        out_shape=out_shape,
        scratch_shapes=[pltpu.VMEM((N_HEADS, LANES), F32)],
        compiler_params=pltpu.CompilerParams(
            dimension_semantics=("arbitrary", "arbitrary"), vmem_limit_bytes=VMEM_LIMIT),
        name="proj_prompt" if prompt else "proj_sample",
    )(x2d, lw["ln1"], lw["wt"], lw["wn"], lw["bf"], lw["gq"], lw["gk"], lw["selq"], lw["selk"])


def _attn_prompt_kernel(qt_ref, qaug_ref, k_ref, kaug_ref, vt_ref, o_ref):
    nt, _, tq = qt_ref.shape
    tk = k_ref.shape[1]
    pair = pl.program_id(1)
    row128 = lax.broadcasted_iota(jnp.int32, (2 * HEAD_DIM, tq), 0)
    ones_rows = jnp.ones((AUG, tk), BF16)
    causal = (lax.broadcasted_iota(jnp.int32, (tk, tq), 0)
              <= lax.broadcasted_iota(jnp.int32, (tk, tq), 1))

    for qi in range(nt):
        q2 = qt_ref[qi]
        aug = qaug_ref[qi]
        zero = jnp.zeros_like(q2)
        qa = []
        for j in range(2):
            top = jnp.where((row128 < HEAD_DIM) == (j == 0), q2, zero)
            bot = jnp.where((row128 // AUG) == (2 * pair + j), aug, zero)
            qa.append(jnp.concatenate([top, bot], axis=0))

        def tile(kj, state, masked):
            kk = jnp.concatenate([k_ref[kj], kaug_ref[kj]], axis=1)
            vt2 = vt_ref[kj]
            new_state = []
            for j in range(2):
                m, acc = state[j]
                s = jnp.dot(kk, qa[j], preferred_element_type=F32)
                if masked:
                    s = jnp.where(causal, s, -jnp.inf)
                m_new = jnp.maximum(m, jnp.max(s, axis=0, keepdims=True))
                p = jnp.exp(s - m_new).astype(BF16)
                alpha = jnp.exp(m - m_new)
                va = jnp.concatenate([vt2[j * HEAD_DIM:(j + 1) * HEAD_DIM], ones_rows], axis=0)
                acc = acc * alpha + jnp.dot(va, p, preferred_element_type=F32)
                new_state.append((m_new, acc))
            return tuple(new_state)

        init = tuple((jnp.full((1, tq), -jnp.inf, F32), jnp.zeros((HEAD_DIM + AUG, tq), F32))
                     for _ in range(2))
        state = tile(qi, init, True)
        if qi > 0:
            state = lax.fori_loop(0, qi, lambda kj, st: tile(kj, st, False), state)
        outs = []
        for j in range(2):
            _, acc = state[j]
            outs.append(acc[:HEAD_DIM] * (1.0 / acc[HEAD_DIM:HEAD_DIM + 1]))
        o_ref[qi] = jnp.concatenate(outs, axis=0).T


def _attn_prompt_call(qt, qaug, kbf, kaug, vt, n_seq, seq_len):
    tm = ROW_TILE
    nt = seq_len // tm
    k4 = kbf.reshape(n_seq, nt, tm, ATTN_WIDTH)
    kaug4 = kaug.reshape(n_seq, nt, tm, LANES)
    out = pl.pallas_call(
        _attn_prompt_kernel,
        grid=(n_seq, N_HEADS // 2),
        in_specs=[
            pl.BlockSpec((None, nt, 2 * HEAD_DIM, tm), lambda b, p: (b, 0, p, 0)),
            pl.BlockSpec((None, nt, N_HEADS * AUG, tm), lambda b, p: (b, 0, 0, 0)),
            pl.BlockSpec((None, nt, tm, 2 * HEAD_DIM), lambda b, p: (b, 0, 0, p)),
            pl.BlockSpec((None, nt, tm, LANES), lambda b, p: (b, 0, 0, 0)),
            pl.BlockSpec((None, nt, 2 * HEAD_DIM, tm), lambda b, p: (b, 0, p, 0)),
        ],
        out_specs=pl.BlockSpec((None, nt, tm, 2 * HEAD_DIM), lambda b, p: (b, 0, 0, p)),
        out_shape=jax.ShapeDtypeStruct((n_seq, nt, tm, ATTN_WIDTH), F32),
        compiler_params=pltpu.CompilerParams(
            dimension_semantics=("arbitrary", "arbitrary"), vmem_limit_bytes=VMEM_LIMIT),
        name="attn_prompt",
    )(qt, qaug, k4, kaug4, vt)
    return out.reshape(n_seq * seq_len, ATTN_WIDTH)


def _attn_sample_kernel(qn_ref, qaug_ref, kn_ref, kaug_ref, vn_ref, cn_ref,
                        kp_ref, vp_ref, lfp_ref, o_ref):
    tq = qn_ref.shape[0]
    past = kp_ref.shape[0]
    nt_dims = (((1,), (1,)), ((), ()))

    lfp = lfp_ref[...]
    lane = lax.broadcasted_iota(jnp.int32, lfp.shape, 1)
    suf = lfp
    d = 1
    while d < past:
        suf = suf + jnp.where(lane < past - d, pltpu.roll(suf, past - d, axis=1), 0.0)
        d *= 2
    dsuf = suf - lfp

    qaug = qaug_ref[...]
    kaug = kaug_ref[...]
    cn = cn_ref[...]
    lane128 = lax.broadcasted_iota(jnp.int32, (tq, 2 * HEAD_DIM), 1)
    causal = (lax.broadcasted_iota(jnp.int32, (tq, tq), 1)
              <= lax.broadcasted_iota(jnp.int32, (tq, tq), 0))
    for pr in range(N_HEADS // 2):
        cols = slice(pr * 2 * HEAD_DIM, (pr + 1) * 2 * HEAD_DIM)
        q2 = qn_ref[:, cols]
        kn2 = kn_ref[:, cols]
        vn2 = vn_ref[:, cols]
        kp2 = kp_ref[:, cols].astype(BF16)
        vp2 = vp_ref[:, cols].astype(BF16)
        zero = jnp.zeros_like(q2)
        outs = []
        for j in range(2):
            hd = 2 * pr + j
            qh = jnp.where((lane128 < HEAD_DIM) == (j == 0), q2, zero)
            qa = jnp.where((lane128 // AUG) == hd, qaug, jnp.zeros_like(qaug))
            s_past = lax.dot_general(qh, kp2, nt_dims, preferred_element_type=F32)
            s_past = s_past + cn[:, hd:hd + 1] + dsuf[hd:hd + 1, :]
            s_new = lax.dot_general(jnp.concatenate([qh, qa], axis=1),
                                    jnp.concatenate([kn2, kaug], axis=1),
                                    nt_dims, preferred_element_type=F32)
            s_new = jnp.where(causal, s_new, -jnp.inf)
            m = jnp.maximum(jnp.max(s_past, axis=1, keepdims=True),
                            jnp.max(s_new, axis=1, keepdims=True))
            p_past = jnp.exp(s_past - m)
            p_new = jnp.exp(s_new - m)
            l = jnp.sum(p_past, axis=1, keepdims=True) + jnp.sum(p_new, axis=1, keepdims=True)
            o = (jnp.dot(p_past.astype(BF16), vp2, preferred_element_type=F32)
                 + jnp.dot(p_new.astype(BF16), vn2, preferred_element_type=F32))
            outs.append(o * (1.0 / l))
        o_ref[:, cols] = jnp.where(lane128 < HEAD_DIM, outs[0], outs[1])


def _attn_sample_call(qn, qaugn, kbf, kaug, vbf, cnat, cache_k, cache_v, cache_logf_t,
                      n_seq, seq_len):
    past = cache_k.shape[1]
    row = lambda w: pl.BlockSpec((seq_len, w), lambda b: (b, 0))
    return pl.pallas_call(
        _attn_sample_kernel,
        grid=(n_seq,),
        in_specs=[
            row(ATTN_WIDTH), row(LANES), row(ATTN_WIDTH), row(LANES), row(ATTN_WIDTH), row(N_HEADS),
            pl.BlockSpec((None, past, ATTN_WIDTH), lambda b: (b, 0, 0)),
            pl.BlockSpec((None, past, ATTN_WIDTH), lambda b: (b, 0, 0)),
            pl.BlockSpec((None, N_HEADS, past), lambda b: (b, 0, 0)),
        ],
        out_specs=row(ATTN_WIDTH),
        out_shape=jax.ShapeDtypeStruct((n_seq * seq_len, ATTN_WIDTH), F32),
        compiler_params=pltpu.CompilerParams(
            dimension_semantics=("arbitrary",), vmem_limit_bytes=VMEM_LIMIT),
        name="attn_sample",
    )(qn, qaugn, kbf, kaug, vbf, cnat, cache_k, cache_v, cache_logf_t)


def _lru_kernel(xr_ref, yg_ref, h0_ref, conv0_ref, cw_ref, cb_ref, wg_ref, bga_ref, bgx_ref,
                lam_ref, glo_ref, out_ref, hlast_ref, convn_ref, xp_scr, h_scr):
    tl = xr_ref.shape[0]
    t = pl.program_id(1)
    last = pl.num_programs(1) - 1

    @pl.when(t == 0)
    def _():
        xp_scr[0:SUBLANES, :] = conv0_ref[...]
        h_scr[...] = h0_ref[...]

    x = xr_ref[...]
    xp_scr[SUBLANES:SUBLANES + tl, :] = x
    xc = cb_ref[...] + x * cw_ref[CONV_W - 1:CONV_W, :]
    for j in range(CONV_W - 1):
        sh = CONV_W - 1 - j
        xc = xc + xp_scr[SUBLANES - sh:SUBLANES - sh + tl, :] * cw_ref[j:j + 1, :]
    tail = xp_scr[tl:tl + SUBLANES, :]
    xp_scr[0:SUBLANES, :] = tail

    xcb = xc.astype(BF16)
    half = LRU_WIDTH // 2
    g0 = jnp.dot(xcb[:, :half], wg_ref[0], preferred_element_type=F32)
    g1 = jnp.dot(xcb[:, half:], wg_ref[1], preferred_element_type=F32)
    ga = jnp.concatenate([g0[:, :half], g1[:, :half]], axis=1) + bga_ref[...]
    gx = jnp.concatenate([g0[:, half:], g1[:, half:]], axis=1) + bgx_ref[...]
    r = jax.nn.sigmoid(ga)
    i = jax.nn.sigmoid(gx)
    nlam = -lam_ref[...]
    sp = jnp.maximum(nlam, 0.0) + jnp.log1p(jnp.exp(-jnp.abs(nlam)))
    log_a = (-RGLRU_C) * r * sp
    a = jnp.exp(log_a)
    bt = jnp.sqrt(1.0 - a * a) * (i * xc)

    row = lax.broadcasted_iota(jnp.int32, (tl, LRU_WIDTH), 0)
    d = 1
    while d < tl:
        valid = row >= d
        a_sh = pltpu.roll(a, d, axis=0)
        b_sh = pltpu.roll(bt, d, axis=0)
        bt = bt + jnp.where(valid, a * b_sh, 0.0)
        a = jnp.where(valid, a * a_sh, a)
        d *= 2
    hs = bt + a * h_scr[...]
    h_scr[...] = hs[tl - 1:tl, :]

    yg = yg_ref[...]
    gelu = 0.5 * yg * (1.0 + jnp.tanh(math.sqrt(2.0 / math.pi) * (yg + 0.044715 * (yg * yg * yg))))
    lru = gelu * hs
    ms = jnp.mean(lru * lru, axis=-1, keepdims=True)
    out_ref[...] = (lru * lax.rsqrt(ms + EPS) * glo_ref[...]).astype(out_ref.dtype)

    @pl.when(t == last)
    def _():
        hlast_ref[...] = hs[tl - 1:tl, :]
        convn_ref[...] = tail[SUBLANES - (CONV_W - 1):, :]


def _lru_call(xr, yg, h0, conv0, lw, n_seq, seq_len):
    tl = min(LRU_TILE, seq_len)
    nt = seq_len // tl
    row_map = lambda b, t: (b * nt + t, 0)
    return pl.pallas_call(
        _lru_kernel,
        grid=(n_seq, nt),
        in_specs=[
            pl.BlockSpec((tl, LRU_WIDTH), row_map),
            pl.BlockSpec((tl, LRU_WIDTH), row_map),
            pl.BlockSpec((None, 1, LRU_WIDTH), lambda b, t: (b, 0, 0)),
            pl.BlockSpec((None, SUBLANES, LRU_WIDTH), lambda b, t: (b, 0, 0)),
            _resident((CONV_W, LRU_WIDTH)),
            _resident((1, LRU_WIDTH)),
            _resident(lw["wg"].shape),
            _resident((1, LRU_WIDTH)),
            _resident((1, LRU_WIDTH)),
            _resident((1, LRU_WIDTH)),
            _resident((1, LRU_WIDTH)),
        ],
        out_specs=[
            pl.BlockSpec((tl, LRU_WIDTH), row_map),
            pl.BlockSpec((None, 1, LRU_WIDTH), lambda b, t: (b, 0, 0)),
            pl.BlockSpec((None, CONV_W - 1, LRU_WIDTH), lambda b, t: (b, 0, 0)),
        ],
        out_shape=[
            jax.ShapeDtypeStruct((n_seq * seq_len, LRU_WIDTH), BF16),
            jax.ShapeDtypeStruct((n_seq, 1, LRU_WIDTH), F32),
            jax.ShapeDtypeStruct((n_seq, CONV_W - 1, LRU_WIDTH), F32),
        ],
        scratch_shapes=[pltpu.VMEM((tl + SUBLANES, LRU_WIDTH), F32),
                        pltpu.VMEM((1, LRU_WIDTH), F32)],
        compiler_params=pltpu.CompilerParams(
            dimension_semantics=("arbitrary", "arbitrary"), vmem_limit_bytes=VMEM_LIMIT),
        name="lru",
    )(xr, yg, h0, conv0, lw["conv_w"], lw["conv_b"], lw["wg"], lw["bga"], lw["bgx"],
      lw["lam"], lw["glo"])


def _mlp_kernel(x_ref, attn_ref, lru_ref, gao_ref, wo_ref, ln2_ref, wup_ref, wdn_ref, y_ref):
    at = attn_ref[...]
    ms = jnp.mean(at * at, axis=-1, keepdims=True)
    an = (at * lax.rsqrt(ms + EPS) * gao_ref[...]).astype(BF16)
    mix = (jnp.dot(an, wo_ref[:ATTN_WIDTH, :], preferred_element_type=F32)
           + jnp.dot(lru_ref[...], wo_ref[ATTN_WIDTH:, :], preferred_element_type=F32))
    x1 = x_ref[...] + mix
    ms2 = jnp.mean(x1 * x1, axis=-1, keepdims=True)
    h2 = (x1 * lax.rsqrt(ms2 + EPS) * ln2_ref[...]).astype(BF16)
    acc = x1
    chunk = D_FF // 4
    for j in range(4):
        u = jnp.dot(h2, wup_ref[:, j * chunk:(j + 1) * chunk], preferred_element_type=F32)
        u = jnp.maximum(u, 0.0)
        acc = acc + jnp.dot((u * u).astype(BF16), wdn_ref[j * chunk:(j + 1) * chunk, :],
                            preferred_element_type=F32)
    y_ref[...] = acc


def _mlp_call(x2d, attn, lru_n, lw):
    rows = x2d.shape[0]
    tm = min(ROW_TILE, rows)
    row = lambda w: pl.BlockSpec((tm, w), lambda i: (i, 0))
    return pl.pallas_call(
        _mlp_kernel,
        grid=(rows // tm,),
        in_specs=[
            row(D_MODEL), row(ATTN_WIDTH), row(LRU_WIDTH),
            _resident((1, ATTN_WIDTH)),
            _resident((D_MODEL, D_MODEL)),
            _resident((1, D_MODEL)),
            _resident((D_MODEL, D_FF)),
            _resident((D_FF, D_MODEL)),
        ],
        out_specs=row(D_MODEL),
        out_shape=jax.ShapeDtypeStruct((rows, D_MODEL), F32),
        compiler_params=pltpu.CompilerParams(
            dimension_semantics=("arbitrary",), vmem_limit_bytes=VMEM_LIMIT),
        name="mlp",
    )(x2d, attn, lru_n, lw["gao"], lw["wo"], lw["ln2"], lw["wup"], lw["wdn"])


def _selection_matrices():
    r = jnp.arange(N_HEADS * AUG)[:, None]
    col = jnp.arange(4 * N_HEADS)[None, :]
    head, j = r // AUG, r % AUG
    piece, chead = col // N_HEADS, col % N_HEADS
    selq = ((head == chead) & (j == piece) & (piece < 3)).astype(BF16)
    selk = -((head == chead) & (j == piece + 3) & (piece < 3)).astype(BF16)
    return selq, selk


def _block_diag_half(w, lo):
    out = jnp.zeros((4 * LRU_BLOCK_DIM, 4 * LRU_BLOCK_DIM), w.dtype)
    for n in range(4):
        out = lax.dynamic_update_slice(out, w[lo + n], (n * LRU_BLOCK_DIM, n * LRU_BLOCK_DIM))
    return out


def _layer_params(l, ln1, w_in, b_f, q_gain, k_gain, conv_w, conv_b, w_gate_a, b_gate_a,
                  w_gate_x, b_gate_x, lru_lambda, g_attn_out, g_lru_out, w_out, ln2, w_up, w_down):
    a = ATTN_WIDTH
    w = w_in[l]
    wq, wk, wv = w[:, :a], w[:, a:2 * a], w[:, 2 * a:3 * a]
    wf = w[:, 3 * a:3 * a + N_HEADS]
    wxr = w[:, 3 * a + N_HEADS:3 * a + N_HEADS + LRU_WIDTH]
    wyg = w[:, 3 * a + N_HEADS + LRU_WIDTH:]
    wt = jnp.concatenate([wq, wk, wv, wf, jnp.zeros((D_MODEL, AUG - N_HEADS), F32)], axis=1).T
    selq, selk = _selection_matrices()
    wg = jnp.stack([
        jnp.concatenate([_block_diag_half(w_gate_a[l], 0), _block_diag_half(w_gate_x[l], 0)], axis=1),
        jnp.concatenate([_block_diag_half(w_gate_a[l], 4), _block_diag_half(w_gate_x[l], 4)], axis=1),
    ])
    row = lambda v: v.reshape(1, -1)
    return dict(
        ln1=row(ln1[l]), wt=wt.astype(BF16), wn=jnp.concatenate([wxr, wyg], axis=1).astype(BF16),
        bf=b_f[l].reshape(N_HEADS, 1), gq=q_gain[l].reshape(HEAD_DIM, 1),
        gk=k_gain[l].reshape(HEAD_DIM, 1), selq=selq, selk=selk,
        conv_w=conv_w[l], conv_b=row(conv_b[l]), wg=wg.astype(BF16),
        bga=row(b_gate_a[l]), bgx=row(b_gate_x[l]), lam=row(lru_lambda[l]), glo=row(g_lru_out[l]),
        gao=row(g_attn_out[l]), wo=w_out[l].astype(BF16), ln2=row(ln2[l]),
        wup=w_up[l].astype(BF16), wdn=w_down[l].astype(BF16),
    )


def kernel(x_prompt, x_sample, cache_k, cache_v, cache_logf, state_h, state_conv, ln1, w_in, b_f,
           q_gain, k_gain, conv_w, conv_b, w_gate_a, b_gate_a, w_gate_x, b_gate_x, lru_lambda,
           g_attn_out, g_lru_out, w_out, ln2, w_up, w_down):
    depth = ln1.shape[0]
    nb, seq, _ = x_prompt.shape
    ndb, dseq, _ = x_sample.shape
    past = cache_k.shape[2]

    yp = x_prompt.reshape(nb * seq, D_MODEL)
    ys = x_sample.reshape(ndb * dseq, D_MODEL)
    zero_h = jnp.zeros((nb, 1, LRU_WIDTH), F32)
    zero_conv = jnp.zeros((nb, SUBLANES, LRU_WIDTH), F32)
    conv_pad = jnp.pad(state_conv, ((0, 0), (0, 0), (SUBLANES - (CONV_W - 1), 0), (0, 0)))
    cache_logf_t = jnp.swapaxes(cache_logf, 2, 3)

    outs = {n: [] for n in ("kp", "vp", "fp", "hp", "cp", "ks", "vs", "fs", "hs", "cs")}
    for l in range(depth):
        lw = _layer_params(l, ln1, w_in, b_f, q_gain, k_gain, conv_w, conv_b, w_gate_a, b_gate_a,
                           w_gate_x, b_gate_x, lru_lambda, g_attn_out, g_lru_out, w_out, ln2,
                           w_up, w_down)
        k, v, logf, kbf, kaug, xr, yg, vt, qt, qaugt = _proj_call(yp, lw, nb, seq, True)
        attn = _attn_prompt_call(qt, qaugt, kbf, kaug, vt, nb, seq)
        lru_n, h_last, conv_new = _lru_call(xr, yg, zero_h, zero_conv, lw, nb, seq)
        yp = _mlp_call(yp, attn, lru_n, lw)
        outs["kp"].append(k); outs["vp"].append(v); outs["fp"].append(logf)
        outs["hp"].append(h_last); outs["cp"].append(conv_new)
        k, v, logf, kbf, kaug, xr, yg, vbf, qn, qaugn, cnat = _proj_call(ys, lw, ndb, dseq, False)
        attn = _attn_sample_call(qn, qaugn, kbf, kaug, vbf, cnat,
                                 cache_k[l].reshape(ndb, past, ATTN_WIDTH),
                                 cache_v[l].reshape(ndb, past, ATTN_WIDTH),
                                 cache_logf_t[l], ndb, dseq)
        lru_n, h_last, conv_new = _lru_call(xr, yg, state_h[l].reshape(ndb, 1, LRU_WIDTH),
                                            conv_pad[l], lw, ndb, dseq)
        ys = _mlp_call(ys, attn, lru_n, lw)
        outs["ks"].append(k); outs["vs"].append(v); outs["fs"].append(logf)
        outs["hs"].append(h_last); outs["cs"].append(conv_new)

    st = lambda name, shape: jnp.stack(outs[name]).reshape((depth,) + shape)
    return (
        yp.reshape(nb, seq, D_MODEL),
        ys.reshape(ndb, dseq, D_MODEL),
        st("kp", (nb, seq, N_HEADS, HEAD_DIM)),
        st("vp", (nb, seq, N_HEADS, HEAD_DIM)),
        st("fp", (nb, seq, N_HEADS)),
        st("hp", (nb, LRU_WIDTH)),
        st("cp", (nb, CONV_W - 1, LRU_WIDTH)),
        st("ks", (ndb, dseq, N_HEADS, HEAD_DIM)),
        st("vs", (ndb, dseq, N_HEADS, HEAD_DIM)),
        st("fs", (ndb, dseq, N_HEADS)),
        st("hs", (ndb, LRU_WIDTH)),
        st("cs", (ndb, CONV_W - 1, LRU_WIDTH)),
    )
```

```python
import functools
import math

import jax
import jax.numpy as jnp
from jax import lax
from jax.experimental import pallas as pl
from jax.experimental.pallas import tpu as pltpu

D_MODEL = 1024
N_HEADS = 8
HEAD_DIM = 64
ATTN_WIDTH = N_HEADS * HEAD_DIM
LRU_WIDTH = D_MODEL - ATTN_WIDTH
LRU_BLOCKS = 8
LRU_BLOCK_DIM = LRU_WIDTH // LRU_BLOCKS
CONV_W = 4
RGLRU_C = 8.0
D_FF = 4 * D_MODEL
EPS = 1e-6
ATTN_SCALE = 1.0 / math.sqrt(HEAD_DIM)
LOG2E = math.log2(math.e)

LANES = 128
SUBLANES = 8
AUG = 16
ROW_TILE = 512
LRU_TILE = 256
VMEM_LIMIT = 56 * 1024 * 1024

BF16 = jnp.bfloat16
F32 = jnp.float32


def _layer_block(arr, l):
    nd = arr.ndim - 1
    return pl.BlockSpec((None,) + arr.shape[1:], lambda *_: (l,) + (0,) * nd,
                        pipeline_mode=pl.Buffered(1))


def _const_block(arr):
    nd = arr.ndim
    return pl.BlockSpec(arr.shape, lambda *_: (0,) * nd, pipeline_mode=pl.Buffered(1))


def _split3(c):
    hi = c.astype(BF16).astype(F32)
    r = c - hi
    mid = r.astype(BF16).astype(F32)
    lo = (r - mid).astype(BF16).astype(F32)
    return hi, mid, lo


def _log_sigmoid(x):
    return jnp.minimum(x, 0.0) - jnp.log1p(jnp.exp(-jnp.abs(x)))


def _proj_kernel(*refs, seg, carry, prompt, n_alias):
    (x_ref, ln1_ref, wt_ref, wn_ref, bf_ref, gq_ref, gk_ref, selq_ref, selk_ref) = refs[:9]
    outs = refs[9 + n_alias:]
    k_ref, v_ref, logf_ref, kbf_ref, kaug_ref, xr_ref, yg_ref = outs[:7]
    if prompt:
        vt_ref, qt_ref, qaugt_ref, c_scr = outs[7:]
    else:
        vbf_ref, qn_ref, qaugn_ref, cnat_ref, c_scr = outs[7:]
    tm = x_ref.shape[0]

    x = x_ref[...]
    ms = jnp.mean(x * x, axis=-1, keepdims=True)
    h = (x * lax.rsqrt(ms + EPS) * ln1_ref[...]).astype(BF16)

    pt = lax.dot_general(wt_ref[...], h, (((1,), (1,)), ((), ())), preferred_element_type=F32)
    pn = jnp.dot(h, wn_ref[...], preferred_element_type=F32)
    xr_ref[...] = pn[:, :LRU_WIDTH]
    yg_ref[...] = pn[:, LRU_WIDTH:]

    gq = gq_ref[...] * (ATTN_SCALE * LOG2E)
    gk = gk_ref[...]
    q_heads, k_heads = [], []
    for hd in range(N_HEADS):
        qh = pt[hd * HEAD_DIM:(hd + 1) * HEAD_DIM]
        kh = pt[ATTN_WIDTH + hd * HEAD_DIM:ATTN_WIDTH + (hd + 1) * HEAD_DIM]
        qh = qh * lax.rsqrt(jnp.mean(qh * qh, axis=0, keepdims=True) + EPS) * gq
        kh = kh * lax.rsqrt(jnp.mean(kh * kh, axis=0, keepdims=True) + EPS) * gk
        q_heads.append(qh)
        k_heads.append(kh)
    qt = jnp.concatenate(q_heads, axis=0)
    kt = jnp.concatenate(k_heads, axis=0)
    vt = pt[2 * ATTN_WIDTH:3 * ATTN_WIDTH]
    ft = pt[3 * ATTN_WIDTH:3 * ATTN_WIDTH + N_HEADS]

    logf_t = _log_sigmoid(ft + bf_ref[...])

    pos = lax.broadcasted_iota(jnp.int32, (N_HEADS, tm), 1) & (seg - 1)
    c = logf_t
    d = 1
    while d < seg:
        c = c + jnp.where(pos >= d, pltpu.roll(c, d, axis=1), 0.0)
        d *= 2
    if carry:
        @pl.when(pl.program_id(1) == 0)
        def _():
            c_scr[...] = jnp.zeros_like(c_scr)
        c = c + c_scr[:, 0:1]
        c_scr[...] = jnp.broadcast_to(c[:, tm - 1:tm], c_scr.shape)
    c2 = c * LOG2E

    hi, mid, lo = _split3(c2)
    parts = jnp.concatenate([hi, mid, lo, jnp.zeros_like(hi)], axis=0).astype(BF16)
    rr = lax.broadcasted_iota(jnp.int32, (N_HEADS * AUG, tm), 0) & (AUG - 1)
    qaug_t = jnp.dot(selq_ref[...], parts, preferred_element_type=F32)
    qaug_t = qaug_t + jnp.where((rr >= 3) & (rr < 6), 1.0, 0.0)
    kaug_t = jnp.dot(selk_ref[...], parts, preferred_element_type=F32)
    kaug_t = kaug_t + jnp.where(rr < 3, 1.0, 0.0)

    k_nat = kt.T
    kbf_ref[...] = k_nat.astype(BF16)
    kaug_ref[...] = kaug_t.T.astype(BF16)

    if prompt:
        k_ref[...] = kt
        v_ref[...] = vt
        logf_ref[...] = logf_t
        vt_ref[...] = vt.astype(BF16)
        qt_ref[...] = qt.astype(BF16)
        qaugt_ref[...] = qaug_t.astype(BF16)
    else:
        v_nat = vt.T
        k_ref[...] = k_nat.reshape(tm, N_HEADS, HEAD_DIM)
        v_ref[...] = v_nat.reshape(tm, N_HEADS, HEAD_DIM)
        slab = jnp.concatenate(
            [logf_t, c2, jnp.zeros((LANES - 2 * N_HEADS, tm), F32)], axis=0).T
        logf_ref[...] = slab[:, :N_HEADS]
        cnat_ref[...] = slab[:, N_HEADS:2 * N_HEADS]
        vbf_ref[...] = v_nat.astype(BF16)
        qn_ref[...] = qt.T.astype(BF16)
        qaugn_ref[...] = qaug_t.T.astype(BF16)


def _proj_call(x2d, pw, l, depth, n_seq, seq_len, prompt, kv_prev):
    rows = n_seq * seq_len
    if prompt:
        tm = ROW_TILE
        nt = seq_len // tm
        grid = (n_seq, nt)
        seg, carry = tm, True
        row_idx = lambda b, t: b * nt + t
        t_map = lambda b, t: (b, t, 0, 0)
    else:
        tm = rows
        nt = 1
        grid = (1, 1)
        seg, carry = seq_len, False
        row_idx = lambda b, t: 0
    assert seg & (seg - 1) == 0 and rows % tm == 0

    def row_spec(width):
        return pl.BlockSpec((tm, width), lambda b, t: (row_idx(b, t), 0))

    if prompt:
        kv_spec = pl.BlockSpec((None, None, ATTN_WIDTH, tm), lambda b, t: (l, b, 0, t))
        kv_shape = jax.ShapeDtypeStruct((depth, n_seq, ATTN_WIDTH, seq_len), F32)
        logf_spec = pl.BlockSpec((None, N_HEADS, tm), lambda b, t: (b, 0, t))
        logf_shape = jax.ShapeDtypeStruct((n_seq, N_HEADS, seq_len), F32)
    else:
        kv_spec = pl.BlockSpec((None, tm, N_HEADS, HEAD_DIM), lambda b, t: (l, 0, 0, 0))
        kv_shape = jax.ShapeDtypeStruct((depth, rows, N_HEADS, HEAD_DIM), F32)
        logf_spec = row_spec(N_HEADS)
        logf_shape = jax.ShapeDtypeStruct((rows, N_HEADS), F32)
    inputs = [x2d, pw["ln1"], pw["wt"], pw["wn"], pw["bf"], pw["gq"], pw["gk"]]
    in_specs = [row_spec(D_MODEL)] + [_layer_block(a, l) for a in inputs[1:]]
    inputs += [pw["selq"], pw["selk"]]
    in_specs += [_const_block(pw["selq"]), _const_block(pw["selk"])]
    aliases = {}
    if kv_prev is not None:
        aliases = {len(inputs): 0, len(inputs) + 1: 1}
        inputs += list(kv_prev)
        in_specs += [pl.BlockSpec(memory_space=pl.ANY)] * 2

    out_shape = [
        kv_shape, kv_shape, logf_shape,
        jax.ShapeDtypeStruct((rows, ATTN_WIDTH), BF16),
        jax.ShapeDtypeStruct((rows, LANES), BF16),
        jax.ShapeDtypeStruct((rows, LRU_WIDTH), F32),
        jax.ShapeDtypeStruct((rows, LRU_WIDTH), F32),
    ]
    out_specs = [kv_spec, kv_spec, logf_spec,
                 row_spec(ATTN_WIDTH), row_spec(LANES), row_spec(LRU_WIDTH), row_spec(LRU_WIDTH)]
    if prompt:
        out_shape += [
            jax.ShapeDtypeStruct((n_seq, nt, ATTN_WIDTH, tm), BF16),
            jax.ShapeDtypeStruct((n_seq, nt, ATTN_WIDTH, tm), BF16),
            jax.ShapeDtypeStruct((n_seq, nt, N_HEADS * AUG, tm), BF16),
        ]
        out_specs += [pl.BlockSpec((None, None, ATTN_WIDTH, tm), t_map),
                      pl.BlockSpec((None, None, ATTN_WIDTH, tm), t_map),
                      pl.BlockSpec((None, None, N_HEADS * AUG, tm), t_map)]
    else:
        out_shape += [
            jax.ShapeDtypeStruct((rows, ATTN_WIDTH), BF16),
            jax.ShapeDtypeStruct((rows, ATTN_WIDTH), BF16),
            jax.ShapeDtypeStruct((rows, LANES), BF16),
            jax.ShapeDtypeStruct((rows, N_HEADS), F32),
        ]
        out_specs += [row_spec(ATTN_WIDTH), row_spec(ATTN_WIDTH), row_spec(LANES), row_spec(N_HEADS)]

    return pl.pallas_call(
        functools.partial(_proj_kernel, seg=seg, carry=carry, prompt=prompt, n_alias=len(aliases)),
        grid=grid,
        in_specs=in_specs,
        out_specs=out_specs,
        out_shape=out_shape,
        input_output_aliases=aliases,
        scratch_shapes=[pltpu.VMEM((N_HEADS, LANES), F32)],
        compiler_params=pltpu.CompilerParams(
            dimension_semantics=("arbitrary", "arbitrary"), vmem_limit_bytes=VMEM_LIMIT),
        name="proj_prompt" if prompt else "proj_sample",
    )(*inputs)


def _attn_prompt_kernel(qt_ref, qaug_ref, k_ref, kaug_ref, vt_ref, o_ref, s_scr):
    nt, _, tq = qt_ref.shape
    tk = k_ref.shape[1]
    pair = pl.program_id(1)
    row128 = lax.broadcasted_iota(jnp.int32, (2 * HEAD_DIM, tq), 0)
    ones_rows = jnp.ones((AUG, tk), BF16)
    causal = (lax.broadcasted_iota(jnp.int32, (tk, tq), 0)
              <= lax.broadcasted_iota(jnp.int32, (tk, tq), 1))

    sched = []
    for qi in range(nt):
        sched.append((qi, qi))
        sched += [(qi, kj) for kj in range(qi)]

    def q_operands(qi):
        q2 = qt_ref[qi]
        aug = qaug_ref[qi]
        zero = jnp.zeros_like(q2)
        qa = []
        for j in range(2):
            top = jnp.where((row128 < HEAD_DIM) == (j == 0), q2, zero)
            bot = jnp.where((row128 // AUG) == (2 * pair + j), aug, zero)
            qa.append(jnp.concatenate([top, bot], axis=0))
        return qa

    def scores(n, qa):
        qi, kj = sched[n]
        kk = jnp.concatenate([k_ref[kj], kaug_ref[kj]], axis=1)
        for j in range(2):
            s = jnp.dot(kk, qa[j], preferred_element_type=F32)
            if kj == qi:
                s = jnp.where(causal, s, -jnp.inf)
            s_scr[n % 2, j] = s

    def fold(n, state):
        _, kj = sched[n]
        vt2 = vt_ref[kj]
        new_state = []
        for j in range(2):
            m, acc = state[j]
            s = s_scr[n % 2, j]
            m_new = jnp.maximum(m, jnp.max(s, axis=0, keepdims=True))
            p = jnp.exp2(s - m_new).astype(BF16)
            alpha = jnp.exp2(m - m_new)
            va = jnp.concatenate([vt2[j * HEAD_DIM:(j + 1) * HEAD_DIM], ones_rows], axis=0)
            acc = acc * alpha + jnp.dot(va, p, preferred_element_type=F32)
            new_state.append((m_new, acc))
        return new_state

    def fresh():
        return [(jnp.full((1, tq), -jnp.inf, F32), jnp.zeros((HEAD_DIM + AUG, tq), F32))
                for _ in range(2)]

    qa_next = q_operands(0)
    scores(0, qa_next)
    state = fresh()
    for n, (qi, kj) in enumerate(sched):
        if n + 1 < len(sched):
            if sched[n + 1][0] != qi:
                qa_next = q_operands(sched[n + 1][0])
            scores(n + 1, qa_next)
        state = fold(n, state)
        if n + 1 == len(sched) or sched[n + 1][0] != qi:
            outs = [acc[:HEAD_DIM] * (1.0 / acc[HEAD_DIM:HEAD_DIM + 1]) for _, acc in state]
            o_ref[qi] = jnp.concatenate(outs, axis=0).T
            state = fresh()


def _attn_prompt_call(qt, qaug, kbf, kaug, vt, n_seq, seq_len):
    tm = ROW_TILE
    nt = seq_len // tm
    k4 = kbf.reshape(n_seq, nt, tm, ATTN_WIDTH)
    kaug4 = kaug.reshape(n_seq, nt, tm, LANES)
    out = pl.pallas_call(
        _attn_prompt_kernel,
        grid=(n_seq, N_HEADS // 2),
        in_specs=[
            pl.BlockSpec((None, nt, 2 * HEAD_DIM, tm), lambda b, p: (b, 0, p, 0)),
            pl.BlockSpec((None, nt, N_HEADS * AUG, tm), lambda b, p: (b, 0, 0, 0)),
            pl.BlockSpec((None, nt, tm, 2 * HEAD_DIM), lambda b, p: (b, 0, 0, p)),
            pl.BlockSpec((None, nt, tm, LANES), lambda b, p: (b, 0, 0, 0)),
            pl.BlockSpec((None, nt, 2 * HEAD_DIM, tm), lambda b, p: (b, 0, p, 0)),
        ],
        out_specs=pl.BlockSpec((None, nt, tm, 2 * HEAD_DIM), lambda b, p: (b, 0, 0, p)),
        out_shape=jax.ShapeDtypeStruct((n_seq, nt, tm, ATTN_WIDTH), F32),
        scratch_shapes=[pltpu.VMEM((2, 2, tm, tm), F32)],
        compiler_params=pltpu.CompilerParams(
            dimension_semantics=("arbitrary", "arbitrary"), vmem_limit_bytes=VMEM_LIMIT),
        name="attn_prompt",
    )(qt, qaug, k4, kaug4, vt)
    return out.reshape(n_seq * seq_len, ATTN_WIDTH)


def _attn_sample_kernel(qn_ref, qaug_ref, kn_ref, kaug_ref, vn_ref, cn_ref,
                        kp_ref, vp_ref, lfp_ref, o_ref):
    tq = qn_ref.shape[0]
    past = kp_ref.shape[0]
    nt_dims = (((1,), (1,)), ((), ()))

    lfp = lfp_ref[...] * LOG2E
    lane = lax.broadcasted_iota(jnp.int32, lfp.shape, 1)
    suf = lfp
    d = 1
    while d < past:
        suf = suf + jnp.where(lane < past - d, pltpu.roll(suf, past - d, axis=1), 0.0)
        d *= 2
    dsuf = suf - lfp

    qaug = qaug_ref[...]
    kaug = kaug_ref[...]
    cn = cn_ref[...]
    lane128 = lax.broadcasted_iota(jnp.int32, (tq, 2 * HEAD_DIM), 1)
    causal = (lax.broadcasted_iota(jnp.int32, (tq, tq), 1)
              <= lax.broadcasted_iota(jnp.int32, (tq, tq), 0))
    for pr in range(N_HEADS // 2):
        cols = slice(pr * 2 * HEAD_DIM, (pr + 1) * 2 * HEAD_DIM)
        q2 = qn_ref[:, cols]
        kn2 = kn_ref[:, cols]
        vn2 = vn_ref[:, cols]
        kp2 = kp_ref[:, cols].astype(BF16)
        vp2 = vp_ref[:, cols].astype(BF16)
        zero = jnp.zeros_like(q2)
        outs = []
        for j in range(2):
            hd = 2 * pr + j
            qh = jnp.where((lane128 < HEAD_DIM) == (j == 0), q2, zero)
            qa = jnp.where((lane128 // AUG) == hd, qaug, jnp.zeros_like(qaug))
            s_past = lax.dot_general(qh, kp2, nt_dims, preferred_element_type=F32)
            s_past = s_past + cn[:, hd:hd + 1] + dsuf[hd:hd + 1, :]
            s_new = lax.dot_general(jnp.concatenate([qh, qa], axis=1),
                                    jnp.concatenate([kn2, kaug], axis=1),
                                    nt_dims, preferred_element_type=F32)
            s_new = jnp.where(causal, s_new, -jnp.inf)
            m = jnp.maximum(jnp.max(s_past, axis=1, keepdims=True),
                            jnp.max(s_new, axis=1, keepdims=True))
            p_past = jnp.exp2(s_past - m)
            p_new = jnp.exp2(s_new - m)
            l = jnp.sum(p_past, axis=1, keepdims=True) + jnp.sum(p_new, axis=1, keepdims=True)
            o = (jnp.dot(p_past.astype(BF16), vp2, preferred_element_type=F32)
                 + jnp.dot(p_new.astype(BF16), vn2, preferred_element_type=F32))
            outs.append(o * (1.0 / l))
        o_ref[:, cols] = jnp.where(lane128 < HEAD_DIM, outs[0], outs[1])


def _attn_sample_call(qn, qaugn, kbf, kaug, vbf, cnat, cache_k, cache_v, cache_logf_t,
                      n_seq, seq_len):
    past = cache_k.shape[1]
    row = lambda w: pl.BlockSpec((seq_len, w), lambda b: (b, 0))
    return pl.pallas_call(
        _attn_sample_kernel,
        grid=(n_seq,),
        in_specs=[
            row(ATTN_WIDTH), row(LANES), row(ATTN_WIDTH), row(LANES), row(ATTN_WIDTH), row(N_HEADS),
            pl.BlockSpec((None, past, ATTN_WIDTH), lambda b: (b, 0, 0)),
            pl.BlockSpec((None, past, ATTN_WIDTH), lambda b: (b, 0, 0)),
            pl.BlockSpec((None, N_HEADS, past), lambda b: (b, 0, 0)),
        ],
        out_specs=row(ATTN_WIDTH),
        out_shape=jax.ShapeDtypeStruct((n_seq * seq_len, ATTN_WIDTH), F32),
        compiler_params=pltpu.CompilerParams(
            dimension_semantics=("arbitrary",), vmem_limit_bytes=VMEM_LIMIT),
        name="attn_sample",
    )(qn, qaugn, kbf, kaug, vbf, cnat, cache_k, cache_v, cache_logf_t)


def _lru_kernel(xr_ref, yg_ref, h0_ref, conv0_ref, cw_ref, cb_ref, wg_ref, bga_ref, bgx_ref,
                lam_ref, glo_ref, out_ref, hlast_ref, convn_ref, xp_scr, h_scr):
    tl = xr_ref.shape[0]
    t = pl.program_id(1)
    last = pl.num_programs(1) - 1

    @pl.when(t == 0)
    def _():
        xp_scr[0:SUBLANES, :] = conv0_ref[...]
        h_scr[...] = h0_ref[...]

    x = xr_ref[...]
    xp_scr[SUBLANES:SUBLANES + tl, :] = x
    xc = cb_ref[...] + x * cw_ref[CONV_W - 1:CONV_W, :]
    for j in range(CONV_W - 1):
        sh = CONV_W - 1 - j
        xc = xc + xp_scr[SUBLANES - sh:SUBLANES - sh + tl, :] * cw_ref[j:j + 1, :]
    tail = xp_scr[tl:tl + SUBLANES, :]
    xp_scr[0:SUBLANES, :] = tail

    xcb = xc.astype(BF16)
    half = LRU_WIDTH // 2
    g0 = jnp.dot(xcb[:, :half], wg_ref[0], preferred_element_type=F32)
    g1 = jnp.dot(xcb[:, half:], wg_ref[1], preferred_element_type=F32)
    ga = jnp.concatenate([g0[:, :half], g1[:, :half]], axis=1) + bga_ref[...]
    gx = jnp.concatenate([g0[:, half:], g1[:, half:]], axis=1) + bgx_ref[...]
    r = jax.nn.sigmoid(ga)
    i = jax.nn.sigmoid(gx)
    nlam = -lam_ref[...]
    sp = jnp.maximum(nlam, 0.0) + jnp.log1p(jnp.exp(-jnp.abs(nlam)))
    log_a = (-RGLRU_C) * r * sp
    a = jnp.exp(log_a)
    bt = jnp.sqrt(1.0 - a * a) * (i * xc)

    groups = tl // SUBLANES
    a3 = a.reshape(groups, SUBLANES, LRU_WIDTH)
    b3 = bt.reshape(groups, SUBLANES, LRU_WIDTH)
    sub = lax.broadcasted_iota(jnp.int32, a3.shape, 1)
    d = 1
    while d < SUBLANES:
        valid = sub >= d
        a_sh = jnp.where(valid, pltpu.roll(a3, d, axis=1), 1.0)
        b_sh = jnp.where(valid, pltpu.roll(b3, d, axis=1), 0.0)
        b3 = b3 + a3 * b_sh
        a3 = a3 * a_sh
        d *= 2
    hrow = h_scr[...]
    hs_groups = []
    for g in range(groups):
        hg = b3[g] + a3[g] * hrow
        hs_groups.append(hg)
        hrow = hg[SUBLANES - 1:SUBLANES, :]
    hs = jnp.concatenate(hs_groups, axis=0)
    h_scr[...] = hrow

    yg = yg_ref[...]
    gelu = 0.5 * yg * (1.0 + jnp.tanh(math.sqrt(2.0 / math.pi) * (yg + 0.044715 * (yg * yg * yg))))
    lru = gelu * hs
    ms = jnp.mean(lru * lru, axis=-1, keepdims=True)
    out_ref[...] = (lru * lax.rsqrt(ms + EPS) * glo_ref[...]).astype(out_ref.dtype)

    @pl.when(t == last)
    def _():
        hlast_ref[...] = hrow
        convn_ref[...] = tail[SUBLANES - (CONV_W - 1):, :]


def _lru_call(xr, yg, h0, conv0, pw, l, n_seq, seq_len):
    tl = min(LRU_TILE, seq_len)
    nt = seq_len // tl
    row_map = lambda b, t: (b * nt + t, 0)
    params = [pw["conv_w"], pw["conv_b"], pw["wg"], pw["bga"], pw["bgx"], pw["lam"], pw["glo"]]
    return pl.pallas_call(
        _lru_kernel,
        grid=(n_seq, nt),
        in_specs=[
            pl.BlockSpec((tl, LRU_WIDTH), row_map),
            pl.BlockSpec((tl, LRU_WIDTH), row_map),
            pl.BlockSpec((None, 1, LRU_WIDTH), lambda b, t: (b, 0, 0)),
            pl.BlockSpec((None, SUBLANES, LRU_WIDTH), lambda b, t: (b, 0, 0)),
        ] + [_layer_block(a, l) for a in params],
        out_specs=[
            pl.BlockSpec((tl, LRU_WIDTH), row_map),
            pl.BlockSpec((None, 1, LRU_WIDTH), lambda b, t: (b, 0, 0)),
            pl.BlockSpec((None, CONV_W - 1, LRU_WIDTH), lambda b, t: (b, 0, 0)),
        ],
        out_shape=[
            jax.ShapeDtypeStruct((n_seq * seq_len, LRU_WIDTH), BF16),
            jax.ShapeDtypeStruct((n_seq, 1, LRU_WIDTH), F32),
            jax.ShapeDtypeStruct((n_seq, CONV_W - 1, LRU_WIDTH), F32),
        ],
        scratch_shapes=[pltpu.VMEM((tl + SUBLANES, LRU_WIDTH), F32),
                        pltpu.VMEM((1, LRU_WIDTH), F32)],
        compiler_params=pltpu.CompilerParams(
            dimension_semantics=("arbitrary", "arbitrary"), vmem_limit_bytes=VMEM_LIMIT),
        name="lru",
    )(xr, yg, h0, conv0, *params)


def _mlp_kernel(x_ref, attn_ref, lru_ref, gao_ref, wo_ref, ln2_ref, wup_ref, wdn_ref, y_ref):
    at = attn_ref[...]
    ms = jnp.mean(at * at, axis=-1, keepdims=True)
    an = (at * lax.rsqrt(ms + EPS) * gao_ref[...]).astype(BF16)
    mix = (jnp.dot(an, wo_ref[:ATTN_WIDTH, :], preferred_element_type=F32)
           + jnp.dot(lru_ref[...], wo_ref[ATTN_WIDTH:, :], preferred_element_type=F32))
    x1 = x_ref[...] + mix
    ms2 = jnp.mean(x1 * x1, axis=-1, keepdims=True)
    h2 = (x1 * lax.rsqrt(ms2 + EPS) * ln2_ref[...]).astype(BF16)
    acc = x1
    chunk = D_FF // 4
    for j in range(4):
        u = jnp.dot(h2, wup_ref[:, j * chunk:(j + 1) * chunk], preferred_element_type=F32)
        u = jnp.maximum(u, 0.0)
        acc = acc + jnp.dot((u * u).astype(BF16), wdn_ref[j * chunk:(j + 1) * chunk, :],
                            preferred_element_type=F32)
    y_ref[...] = acc


def _mlp_call(x2d, attn, lru_n, pw, l):
    rows = x2d.shape[0]
    tm = min(ROW_TILE, rows)
    row = lambda w: pl.BlockSpec((tm, w), lambda i: (i, 0))
    params = [pw["gao"], pw["wo"], pw["ln2"], pw["wup"], pw["wdn"]]
    return pl.pallas_call(
        _mlp_kernel,
        grid=(rows // tm,),
        in_specs=[row(D_MODEL), row(ATTN_WIDTH), row(LRU_WIDTH)]
                 + [_layer_block(a, l) for a in params],
        out_specs=row(D_MODEL),
        out_shape=jax.ShapeDtypeStruct((rows, D_MODEL), F32),
        compiler_params=pltpu.CompilerParams(
            dimension_semantics=("arbitrary",), vmem_limit_bytes=VMEM_LIMIT),
        name="mlp",
    )(x2d, attn, lru_n, *params)


def _selection_matrices():
    r = jnp.arange(N_HEADS * AUG)[:, None]
    col = jnp.arange(4 * N_HEADS)[None, :]
    head, j = r // AUG, r % AUG
    piece, chead = col // N_HEADS, col % N_HEADS
    selq = ((head == chead) & (j == piece) & (piece < 3)).astype(BF16)
    selk = -((head == chead) & (j == piece + 3) & (piece < 3)).astype(BF16)
    return selq, selk


def _gate_weights(w_gate_a, w_gate_x):
    eye = jnp.eye(LRU_BLOCKS, dtype=w_gate_a.dtype)

    def dense(w):
        bd = jnp.einsum("lncd,nm->lncmd", w, eye)
        return bd.reshape(w.shape[0], LRU_WIDTH, LRU_WIDTH)

    da, dx = dense(w_gate_a), dense(w_gate_x)
    half = LRU_WIDTH // 2
    lo = jnp.concatenate([da[:, :half, :half], dx[:, :half, :half]], axis=2)
    hi = jnp.concatenate([da[:, half:, half:], dx[:, half:, half:]], axis=2)
    return jnp.stack([lo, hi], axis=1).astype(BF16)


def _pack_params(ln1, w_in, b_f, q_gain, k_gain, conv_w, conv_b, w_gate_a, b_gate_a, w_gate_x,
                 b_gate_x, lru_lambda, g_attn_out, g_lru_out, w_out, ln2, w_up, w_down):
    depth = ln1.shape[0]
    a = ATTN_WIDTH
    n_t = 3 * a + N_HEADS
    wt = jnp.concatenate(
        [w_in[:, :, :n_t], jnp.zeros((depth, D_MODEL, AUG - N_HEADS), F32)], axis=2)
    wt = jnp.swapaxes(wt, 1, 2).astype(BF16)
    selq, selk = _selection_matrices()
    row = lambda v: v.reshape(depth, 1, -1)
    col = lambda v: v.reshape(depth, -1, 1)
    return dict(
        ln1=row(ln1), wt=wt, wn=w_in[:, :, n_t:].astype(BF16),
        bf=col(b_f), gq=col(q_gain), gk=col(k_gain), selq=selq, selk=selk,
        conv_w=conv_w, conv_b=row(conv_b), wg=_gate_weights(w_gate_a, w_gate_x),
        bga=row(b_gate_a), bgx=row(b_gate_x), lam=row(lru_lambda), glo=row(g_lru_out),
        gao=row(g_attn_out), wo=w_out.astype(BF16), ln2=row(ln2),
        wup=w_up.astype(BF16), wdn=w_down.astype(BF16),
    )


def kernel(x_prompt, x_sample, cache_k, cache_v, cache_logf, state_h, state_conv, ln1, w_in, b_f,
           q_gain, k_gain, conv_w, conv_b, w_gate_a, b_gate_a, w_gate_x, b_gate_x, lru_lambda,
           g_attn_out, g_lru_out, w_out, ln2, w_up, w_down):
    depth = ln1.shape[0]
    nb, seq, _ = x_prompt.shape
    ndb, dseq, _ = x_sample.shape
    past = cache_k.shape[2]
    pw = _pack_params(ln1, w_in, b_f, q_gain, k_gain, conv_w, conv_b, w_gate_a, b_gate_a,
                      w_gate_x, b_gate_x, lru_lambda, g_attn_out, g_lru_out, w_out, ln2,
                      w_up, w_down)

    yp = x_prompt.reshape(nb * seq, D_MODEL)
    ys = x_sample.reshape(ndb * dseq, D_MODEL)
    zero_h = jnp.zeros((nb, 1, LRU_WIDTH), F32)
    zero_conv = jnp.zeros((nb, SUBLANES, LRU_WIDTH), F32)
    conv_pad = jnp.pad(state_conv, ((0, 0), (0, 0), (SUBLANES - (CONV_W - 1), 0), (0, 0)))
    cache_logf_t = jnp.swapaxes(cache_logf, 2, 3)
    cache_k4 = cache_k.reshape(depth, ndb, past, ATTN_WIDTH)
    cache_v4 = cache_v.reshape(depth, ndb, past, ATTN_WIDTH)

    outs = {n: [] for n in ("fp", "hp", "cp", "fs", "hs", "cs")}
    kv_p = kv_s = None
    for l in range(depth):
        kp, vp, logf, kbf, kaug, xr, yg, vt, qt, qaugt = _proj_call(
            yp, pw, l, depth, nb, seq, True, kv_p)
        kv_p = (kp, vp)
        attn = _attn_prompt_call(qt, qaugt, kbf, kaug, vt, nb, seq)
        lru_n, h_last, conv_new = _lru_call(xr, yg, zero_h, zero_conv, pw, l, nb, seq)
        yp = _mlp_call(yp, attn, lru_n, pw, l)
        outs["fp"].append(logf); outs["hp"].append(h_last); outs["cp"].append(conv_new)
        ks, vs, logf, kbf, kaug, xr, yg, vbf, qn, qaugn, cnat = _proj_call(
            ys, pw, l, depth, ndb, dseq, False, kv_s)
        kv_s = (ks, vs)
        attn = _attn_sample_call(qn, qaugn, kbf, kaug, vbf, cnat, cache_k4[l], cache_v4[l],
                                 cache_logf_t[l], ndb, dseq)
        lru_n, h_last, conv_new = _lru_call(xr, yg, state_h[l].reshape(ndb, 1, LRU_WIDTH),
                                            conv_pad[l], pw, l, ndb, dseq)
        ys = _mlp_call(ys, attn, lru_n, pw, l)
        outs["fs"].append(logf); outs["hs"].append(h_last); outs["cs"].append(conv_new)

    st = lambda name, shape: jnp.stack(outs[name]).reshape((depth,) + shape)
    token_major = lambda a: jnp.transpose(
        a.reshape(depth, nb, N_HEADS, HEAD_DIM, seq), (0, 1, 4, 2, 3))
    return (
        yp.reshape(nb, seq, D_MODEL),
        ys.reshape(ndb, dseq, D_MODEL),
        token_major(kv_p[0]),
        token_major(kv_p[1]),
        jnp.swapaxes(jnp.stack(outs["fp"]), 2, 3),
        st("hp", (nb, LRU_WIDTH)),
        st("cp", (nb, CONV_W - 1, LRU_WIDTH)),
        kv_s[0].reshape(depth, ndb, dseq, N_HEADS, HEAD_DIM),
        kv_s[1].reshape(depth, ndb, dseq, N_HEADS, HEAD_DIM),
        st("fs", (ndb, dseq, N_HEADS)),
        st("hs", (ndb, LRU_WIDTH)),
        st("cs", (ndb, CONV_W - 1, LRU_WIDTH)),
    )
```
